```python
import jax, jax.numpy as jnp
from jax import lax
import numpy as np

D_MODEL = 1024
BATCH = 8
SEQ = 2048
DEPTH = 1

CHUNK = 64
Q_BLOCK = 2 * CHUNK
D_MIX = D_MODEL
ATTN_HEADS = 8
HEAD_DIM = 64
D_ATTN = ATTN_HEADS * HEAD_DIM
D_CONV = D_MIX - D_ATTN
CONV_WIDTH = 3
N_EXPERTS = 32
TOP_K = 4
D_EXPERT = D_MODEL
SWIGLU_LIMIT = 7.0
SWIGLU_ALPHA = 1.702
MOE_BLOCK = 128
N_MOD = 6
EPS = 1e-6
SPLITS = (D_ATTN, 2 * D_ATTN, 3 * D_ATTN, 3 * D_ATTN + ATTN_HEADS,
          3 * D_ATTN + ATTN_HEADS + D_CONV, 3 * D_ATTN + ATTN_HEADS + 2 * D_CONV)
D_IN_PROJ = 3 * D_ATTN + ATTN_HEADS + 3 * D_CONV

kernel_name = "hybrid_fox_shortconv_moe_adaln"


def rms_norm(x, w):
    xf = x.astype(jnp.float32)
    y = xf * lax.rsqrt(jnp.mean(xf * xf, axis=-1, keepdims=True) + EPS)
    return (y * w.astype(jnp.float32)).astype(x.dtype)


def forgetting_attention(q, k, v, log_f):
    s_len = q.shape[2]
    cum = jnp.cumsum(log_f, axis=-1)
    scale = HEAD_DIM ** -0.5
    outs = []
    for i in range(s_len // Q_BLOCK):
        q0, q1 = i * Q_BLOCK, (i + 1) * Q_BLOCK
        s = jnp.einsum('bhqd,bhkd->bhqk', q[:, :, q0:q1], k[:, :, :q1],
                       preferred_element_type=jnp.float32) * scale
        s = s + cum[:, :, q0:q1, None] - cum[:, :, None, :q1]
        q_pos = q0 + jnp.arange(Q_BLOCK)
        k_pos = jnp.arange(q1)
        s = jnp.where(k_pos[None, :] <= q_pos[:, None], s, -jnp.inf)
        p = jax.nn.softmax(s, axis=-1)
        outs.append(jnp.einsum('bhqk,bhkd->bhqd', p.astype(v.dtype), v[:, :, :q1]))
    return jnp.concatenate(outs, axis=2)


def causal_short_conv(u, w):
    s_len = u.shape[1]
    up = jnp.pad(u, ((0, 0), (CONV_WIDTH - 1, 0), (0, 0)))
    y = w[0] * up[:, 0:s_len]
    for j in range(1, CONV_WIDTH):
        y = y + w[j] * up[:, j:j + s_len]
    return y


def moe_ffn(h, w_router, b_router, w_gate_up, b_gate_up, w_down, b_down):
    bsz, s_len, d = h.shape
    t = bsz * s_len
    hf = h.reshape(t, d)
    logits = jnp.einsum('td,de->te', hf, w_router, preferred_element_type=jnp.float32) \
        + b_router.astype(jnp.float32)
    top_logits, top_idx = lax.top_k(logits, TOP_K)
    top_w = jax.nn.softmax(top_logits, axis=-1)
    n_assign = t * TOP_K
    flat_e = top_idx.reshape(n_assign)
    order = jnp.argsort(flat_e)
    sorted_e = flat_e[order]
    sorted_tok = (order // TOP_K).astype(jnp.int32)
    sorted_w = top_w.reshape(n_assign)[order]
    counts = jnp.bincount(flat_e, length=N_EXPERTS)
    padded = (counts + MOE_BLOCK - 1) // MOE_BLOCK * MOE_BLOCK
    pad_end = jnp.cumsum(padded)
    pad_start = pad_end - padded
    start = jnp.cumsum(counts) - counts
    dest = pad_start[sorted_e] + (jnp.arange(n_assign) - start[sorted_e])
    n_blocks = -(-n_assign // MOE_BLOCK) + N_EXPERTS
    n_rows = n_blocks * MOE_BLOCK
    row_tok = jnp.zeros((n_rows,), jnp.int32).at[dest].set(sorted_tok)
    row_w = jnp.zeros((n_rows,), jnp.float32).at[dest].set(sorted_w)
    block_e = jnp.minimum(
        jnp.searchsorted(pad_end, jnp.arange(n_blocks) * MOE_BLOCK, side='right'), N_EXPERTS - 1)
    xb = hf[row_tok].reshape(n_blocks, MOE_BLOCK, d)

    def expert_block(args):
        xe, e = args
        gu = xe @ w_gate_up[e] + b_gate_up[e]
        gate, up = jnp.split(gu, 2, axis=-1)
        gate = jnp.minimum(gate, SWIGLU_LIMIT)
        up = jnp.clip(up, -SWIGLU_LIMIT, SWIGLU_LIMIT)
        act = gate * jax.nn.sigmoid(SWIGLU_ALPHA * gate) * (up + 1)
        return act @ w_down[e] + b_down[e]

    yb = lax.map(expert_block, (xb, block_e))
    y_rows = yb.reshape(n_rows, d) * row_w[:, None].astype(yb.dtype)
    return jnp.zeros_like(hf).at[row_tok].add(y_rows).reshape(bsz, s_len, d)


def setup_inputs(seed: int = 0) -> dict:
    key = jax.random.key(seed)
    ks = jax.random.split(key, 20)
    L, D = DEPTH, D_MODEL
    nrm = jax.random.normal
    return {
        "x": nrm(ks[0], (BATCH, SEQ, D), jnp.float32),
        "c": nrm(ks[1], (BATCH, D), jnp.float32),
        "w_ada": nrm(ks[2], (L, D, N_MOD * D), jnp.float32) * (0.5 * D ** -0.5),
        "b_ada": nrm(ks[3], (L, N_MOD * D), jnp.float32) * 0.01,
        "norm1_w": 1.0 + 0.1 * nrm(ks[4], (L, D), jnp.float32),
        "w_in": nrm(ks[5], (L, D, D_IN_PROJ), jnp.float32) * D ** -0.5,
        "b_forget": 3.0 + 0.5 * nrm(ks[6], (L, ATTN_HEADS), jnp.float32),
        "q_norm_w": 1.0 + 0.1 * nrm(ks[7], (L, HEAD_DIM), jnp.float32),
        "k_norm_w": 1.0 + 0.1 * nrm(ks[8], (L, HEAD_DIM), jnp.float32),
        "conv_w": nrm(ks[9], (L, CONV_WIDTH, D_CONV), jnp.float32) * CONV_WIDTH ** -0.5,
        "attn_out_norm_w": 1.0 + 0.1 * nrm(ks[10], (L, D_ATTN), jnp.float32),
        "conv_out_norm_w": 1.0 + 0.1 * nrm(ks[11], (L, D_CONV), jnp.float32),
        "w_out": nrm(ks[12], (L, D_MIX, D), jnp.float32) * D_MIX ** -0.5,
        "norm2_w": 1.0 + 0.1 * nrm(ks[13], (L, D), jnp.float32),
        "w_router": nrm(ks[14], (L, D, N_EXPERTS), jnp.float32) * D ** -0.5,
        "b_router": nrm(ks[15], (L, N_EXPERTS), jnp.float32) * 0.01,
        "w_gate_up": nrm(ks[16], (L, N_EXPERTS, D, 2 * D_EXPERT), jnp.float32) * D ** -0.5,
        "b_gate_up": nrm(ks[17], (L, N_EXPERTS, 2 * D_EXPERT), jnp.float32) * 0.01,
        "w_down": nrm(ks[18], (L, N_EXPERTS, D_EXPERT, D), jnp.float32) * D_EXPERT ** -0.5,
        "b_down": nrm(ks[19], (L, N_EXPERTS, D), jnp.float32) * 0.01,
    }


def reference(x, c, w_ada, b_ada, norm1_w, w_in, b_forget, q_norm_w, k_norm_w, conv_w,
              attn_out_norm_w, conv_out_norm_w, w_out, norm2_w, w_router, b_router,
              w_gate_up, b_gate_up, w_down, b_down):
    bsz, s_len, _ = x.shape

    def to_heads(t):
        return t.reshape(bsz, s_len, ATTN_HEADS, HEAD_DIM).transpose(0, 2, 1, 3)

    for l in range(DEPTH):
        mod = jnp.einsum('bd,de->be', jax.nn.silu(c), w_ada[l]) + b_ada[l]
        sh1, sc1, g1, sh2, sc2, g2 = jnp.split(mod[:, None, :], N_MOD, axis=-1)

        h = rms_norm(x, norm1_w[l]) * (1 + sc1) + sh1
        proj = h @ w_in[l]
        q, k, v, f_logit, gb, gc, u = jnp.split(proj, SPLITS, axis=-1)

        qh = rms_norm(to_heads(q), q_norm_w[l])
        kh = rms_norm(to_heads(k), k_norm_w[l])
        vh = to_heads(v)
        log_f = jax.nn.log_sigmoid((f_logit + b_forget[l]).astype(jnp.float32)).transpose(0, 2, 1)
        attn = forgetting_attention(qh, kh, vh, log_f)
        attn = attn.transpose(0, 2, 1, 3).reshape(bsz, s_len, D_ATTN)

        conv = gb * causal_short_conv(gc * u, conv_w[l])

        mixed = jnp.concatenate([rms_norm(attn, attn_out_norm_w[l]),
                                 rms_norm(conv, conv_out_norm_w[l])], axis=-1)
        x = x + g1 * (mixed @ w_out[l])

        h2 = rms_norm(x, norm2_w[l]) * (1 + sc2) + sh2
        x = x + g2 * moe_ffn(h2, w_router[l], b_router[l], w_gate_up[l], b_gate_up[l],
                             w_down[l], b_down[l])
    return x
```

```python
import functools

import jax
import jax.numpy as jnp
from jax import lax
from jax.experimental import pallas as pl
from jax.experimental.pallas import tpu as pltpu

D_MODEL = 1024
ATTN_HEADS = 8
HEAD_DIM = 64
D_ATTN = ATTN_HEADS * HEAD_DIM
D_CONV = D_MODEL - D_ATTN
CONV_WIDTH = 3
N_EXPERTS = 32
TOP_K = 4
D_EXPERT = D_MODEL
SWIGLU_LIMIT = 7.0
SWIGLU_ALPHA = 1.702
N_MOD = 6
EPS = 1e-6

LANES = 128
N_BIAS_TERMS = 6
NEG = -1e30
VMEM_LIMIT = 56 * 1024 * 1024

TM_INPROJ = 512
TQ = 256
TM_MIX = 256
TM_ROUTE = 256
BM = 512

f32 = jnp.float32
bf16 = jnp.bfloat16


def _dot(a, b):
    return jnp.dot(a, b, preferred_element_type=f32)


def _split2(a):
    hi = a.astype(bf16)
    lo = (a - hi.astype(f32)).astype(bf16)
    return hi, lo


def _split3(a):
    hi = a.astype(bf16)
    r = a - hi.astype(f32)
    mid = r.astype(bf16)
    lo = (r - mid.astype(f32)).astype(bf16)
    return hi, mid, lo


def _rms(x):
    return x * lax.rsqrt(jnp.mean(x * x, axis=-1, keepdims=True) + EPS)


def _ada_body(c_ref, w_ref, b_ref, o_ref):
    c = c_ref[...]
    s = c * jax.nn.sigmoid(c)
    sh, sl = _split2(s)
    wh, wl = _split2(w_ref[...])
    o_ref[...] = _dot(sh, wh) + _dot(sl, wh) + _dot(sh, wl) + b_ref[...]


def _ada(c_pad, w_ada, b_ada):
    rows, d = c_pad.shape
    n = w_ada.shape[1]
    bn = n // 4
    return pl.pallas_call(
        _ada_body,
        grid=(n // bn,),
        in_specs=[pl.BlockSpec((rows, d), lambda j: (0, 0)),
                  pl.BlockSpec((d, bn), lambda j: (0, j)),
                  pl.BlockSpec((1, bn), lambda j: (0, j))],
        out_specs=pl.BlockSpec((rows, bn), lambda j: (0, j)),
        out_shape=jax.ShapeDtypeStruct((rows, n), f32),
        compiler_params=pltpu.CompilerParams(dimension_semantics=("arbitrary",), vmem_limit_bytes=VMEM_LIMIT),
        name="ada",
    )(c_pad, w_ada, b_ada)


def _inproj_body(x_ref, mod_ref, n1_ref, wmain_ref, wf_ref, bf_ref, g_ref, qw_ref, kw_ref, cw_ref, cnw_ref, tri_ref,
                 q_ref, k_ref, v_ref, eq_ref, ek_ref, cv_ref, cum_carry, conv_carry):
    tm = x_ref.shape[1]

    @pl.when(pl.program_id(1) == 0)
    def _():
        cum_carry[...] = jnp.zeros_like(cum_carry)
        conv_carry[...] = jnp.zeros_like(conv_carry)

    sh1 = mod_ref[0, 0:1, :]
    sc1 = mod_ref[0, 1:2, :]
    h = _rms(x_ref[0]) * n1_ref[...]
    h = h * (1 + sc1) + sh1
    hb = h.astype(bf16)
    proj = _dot(hb, wmain_ref[...])

    def head_norm(t, w):
        th, tl = _split2(t * t)
        ms = _dot(th, g_ref[...]) + _dot(tl, g_ref[...])
        return t * lax.rsqrt(ms + EPS) * w

    q = head_norm(proj[:, 0:D_ATTN], qw_ref[...]) * (HEAD_DIM ** -0.5)
    k = head_norm(proj[:, D_ATTN:2 * D_ATTN], kw_ref[...])
    q_ref[0] = q.astype(bf16)
    k_ref[0] = k.astype(bf16)
    v_ref[0] = proj[:, 2 * D_ATTN:3 * D_ATTN].astype(bf16)

    fl = _dot(hb, wf_ref[...]) + bf_ref[...]
    logf = jnp.minimum(fl, 0.0) - jnp.log1p(jnp.exp(-jnp.abs(fl)))
    l_hi, l_mid, l_lo = _split3(logf)
    tri = tri_ref[...]
    cum = _dot(tri, l_hi) + _dot(tri, l_mid) + _dot(tri, l_lo) + cum_carry[...]
    cum_carry[...] = cum[tm - 1:tm, :]
    c_hi = cum.astype(bf16).astype(f32)
    r1 = cum - c_hi
    c_mid = r1.astype(bf16).astype(f32)
    c_lo = (r1 - c_mid).astype(bf16).astype(f32)
    lane = lax.broadcasted_iota(jnp.int32, (tm, LANES), 1)
    h8 = ATTN_HEADS
    eq = jnp.where(lane < h8, c_hi, jnp.where(lane < 2 * h8, c_mid, jnp.where(
        lane < 3 * h8, c_lo, jnp.where(lane < 6 * h8, 1.0, 0.0))))
    ek = jnp.where(lane < 3 * h8, 1.0, jnp.where(lane < 4 * h8, -c_hi, jnp.where(
        lane < 5 * h8, -c_mid, jnp.where(lane < 6 * h8, -c_lo, 0.0))))
    eq_ref[0] = eq.astype(bf16)
    ek_ref[0] = ek.astype(bf16)

    o = 3 * D_ATTN
    gb = proj[:, o:o + D_CONV]
    w = proj[:, o + D_CONV:o + 2 * D_CONV] * proj[:, o + 2 * D_CONV:o + 3 * D_CONV]
    prev = conv_carry[...]
    row = lax.broadcasted_iota(jnp.int32, (tm, D_CONV), 0)
    w1 = jnp.where(row == 0, prev[7:8, :], pltpu.roll(w, 1, axis=0))
    w2 = jnp.where(row == 0, prev[6:7, :], jnp.where(row == 1, prev[7:8, :], pltpu.roll(w, 2, axis=0)))
    conv_carry[...] = w[tm - 8:tm, :]
    conv = gb * (cw_ref[0:1, :] * w2 + cw_ref[1:2, :] * w1 + cw_ref[2:3, :] * w)
    cv_ref[0] = (_rms(conv) * cnw_ref[...]).astype(bf16)


def _inproj(x, mod, norm1_w, w_main, w_f, b_f, g_mat, qw, kw, conv_w, conv_nw, tri):
    b, s, d = x.shape
    tm = min(TM_INPROJ, s)
    nmain = w_main.shape[1]
    const = lambda shape: pl.BlockSpec(shape, lambda i, t: (0,) * len(shape))
    tile = lambda n: pl.BlockSpec((1, tm, n), lambda i, t: (i, t, 0))
    outs = [jax.ShapeDtypeStruct((b, s, D_ATTN), bf16)] * 3 + [jax.ShapeDtypeStruct((b, s, LANES), bf16)] * 2 \
        + [jax.ShapeDtypeStruct((b, s, D_CONV), bf16)]
    return pl.pallas_call(
        _inproj_body,
        grid=(b, s // tm),
        in_specs=[tile(d),
                  pl.BlockSpec((1, N_MOD, d), lambda i, t: (i, 0, 0)),
                  const((1, d)), const((d, nmain)), const((d, LANES)), const((1, LANES)),
                  const((D_ATTN, D_ATTN)), const((1, D_ATTN)), const((1, D_ATTN)),
                  const((CONV_WIDTH, D_CONV)), const((1, D_CONV)), const((tm, tm))],
        out_specs=[tile(D_ATTN), tile(D_ATTN), tile(D_ATTN), tile(LANES), tile(LANES), tile(D_CONV)],
        out_shape=outs,
        scratch_shapes=[pltpu.VMEM((1, LANES), f32), pltpu.VMEM((8, D_CONV), f32)],
        compiler_params=pltpu.CompilerParams(dimension_semantics=("arbitrary", "arbitrary"),
                                             vmem_limit_bytes=VMEM_LIMIT),
        name="inproj",
    )(x, mod, norm1_w, w_main, w_f, b_f, g_mat, qw, kw, conv_w, conv_nw, tri)


def _attn_body(q_ref, k_ref, v_ref, eq_ref, ek_ref, o_ref):
    tq = q_ref.shape[1]
    i = pl.program_id(2)
    hp = pl.program_id(1)
    lane = lax.broadcasted_iota(jnp.int32, (tq, LANES), 1)
    q2 = q_ref[0].astype(f32)
    eq = eq_ref[0].astype(f32)
    row = lax.broadcasted_iota(jnp.int32, (tq, tq), 0)
    col = lax.broadcasted_iota(jnp.int32, (tq, tq), 1)
    outs = []
    for hh in range(2):
        head = hp * 2 + hh
        in_head = (lane < HEAD_DIM) if hh == 0 else (lane >= HEAD_DIM)
        qm = jnp.where(in_head, q2, 0.0).astype(bf16)
        em = jnp.where((lane % ATTN_HEADS == head) & (lane < N_BIAS_TERMS * ATTN_HEADS), eq, 0.0).astype(bf16)
        qq = jnp.concatenate([qm, em], axis=-1)

        def step(j, carry, masked):
            m, l, acc = carry
            off = pl.multiple_of(j * tq, tq)
            kk = jnp.concatenate([k_ref[0, pl.ds(off, tq), :], ek_ref[0, pl.ds(off, tq), :]], axis=-1)
            s = lax.dot_general(qq, kk, (((1,), (1,)), ((), ())), preferred_element_type=f32)
            if masked:
                s = jnp.where(col <= row, s, -jnp.inf)
            m_new = jnp.maximum(m, jnp.max(s, axis=-1, keepdims=True))
            p = jnp.exp(s - m_new)
            alpha = jnp.exp(m - m_new)
            l = alpha * l + jnp.sum(p, axis=-1, keepdims=True)
            acc = alpha * acc + _dot(p.astype(bf16), v_ref[0, pl.ds(off, tq), :])
            return m_new, l, acc

        init = (jnp.full((tq, 1), -jnp.inf, f32), jnp.zeros((tq, 1), f32), jnp.zeros((tq, LANES), f32))
        carry = lax.fori_loop(0, i, lambda j, c: step(j, c, False), init)
        m, l, acc = step(i, carry, True)
        outs.append(acc / l)
    o_ref[0] = jnp.where(lane < HEAD_DIM, outs[0], outs[1])


def _attention(q, k, v, eq, ek):
    b, s, _ = q.shape
    tq = min(TQ, s)
    qspec = pl.BlockSpec((1, tq, LANES), lambda i, h, t: (i, t, h))
    kspec = pl.BlockSpec((1, s, LANES), lambda i, h, t: (i, 0, h))
    return pl.pallas_call(
        _attn_body,
        grid=(b, D_ATTN // LANES, s // tq),
        in_specs=[qspec, kspec, kspec,
                  pl.BlockSpec((1, tq, LANES), lambda i, h, t: (i, t, 0)),
                  pl.BlockSpec((1, s, LANES), lambda i, h, t: (i, 0, 0))],
        out_specs=qspec,
        out_shape=jax.ShapeDtypeStruct((b, s, D_ATTN), f32),
        compiler_params=pltpu.CompilerParams(dimension_semantics=("arbitrary",) * 3, vmem_limit_bytes=VMEM_LIMIT),
        name="attn",
    )(q, k, v, eq, ek)


def _mix_body(attn_ref, cv_ref, x_ref, mod_ref, aw_ref, wo_ref, n2_ref, wrh_ref, wrl_ref, br_ref, tril_ref,
              x1_ref, h2_ref, ri_ref, rw_ref, cnt_ref, carry):
    tm = x_ref.shape[0]

    @pl.when(pl.program_id(0) == 0)
    def _():
        carry[...] = jnp.zeros_like(carry)

    g1 = mod_ref[0, 2:3, :]
    sh2 = mod_ref[0, 3:4, :]
    sc2 = mod_ref[0, 4:5, :]
    an = (_rms(attn_ref[...]) * aw_ref[...]).astype(bf16)
    mixed = jnp.concatenate([an, cv_ref[...]], axis=-1)
    x1 = x_ref[...] + g1 * _dot(mixed, wo_ref[...])
    x1_ref[...] = x1
    h2 = _rms(x1) * n2_ref[...]
    h2 = h2 * (1 + sc2) + sh2
    h2_ref[...] = h2

    hh, hl = _split2(h2)
    logits = _dot(hh, wrh_ref[...]) + _dot(hl, wrh_ref[...]) + _dot(hh, wrl_ref[...]) + br_ref[...]
    lane = lax.broadcasted_iota(jnp.int32, (tm, LANES), 1)
    onehots, tops, idxs = [], [], []
    lg = logits
    for _ in range(TOP_K):
        mx = jnp.max(lg, axis=-1, keepdims=True)
        idx = jnp.min(jnp.where(lg == mx, lane, LANES), axis=-1, keepdims=True)
        oh = lane == idx
        lg = jnp.where(oh, 2 * NEG, lg)
        onehots.append(oh)
        tops.append(mx)
        idxs.append(idx)
    es = [jnp.exp(t - tops[0]) for t in tops]
    denom = es[0] + es[1] + es[2] + es[3]

    sel = jnp.zeros((tm, LANES), f32)
    for oh in onehots:
        sel = sel + jnp.where(oh, 1.0, 0.0)
    pos_full = _dot(tril_ref[...], sel.astype(bf16)) + carry[...]
    carry[...] = carry[...] + jnp.sum(sel, axis=0, keepdims=True)
    cnt_ref[...] = carry[...]

    ri = jnp.zeros((tm, LANES), jnp.int32)
    rw = jnp.zeros((tm, LANES), f32)
    for kk in range(TOP_K):
        pos = jnp.sum(jnp.where(onehots[kk], pos_full, 0.0), axis=-1, keepdims=True)
        ri = jnp.where(lane == kk, idxs[kk], ri)
        ri = jnp.where(lane == TOP_K + kk, pos.astype(jnp.int32), ri)
        rw = jnp.where(lane == kk, es[kk] / denom, rw)
    ri_ref[...] = ri
    rw_ref[...] = rw


def _mix(attn, cv, x, mod, aw, wo, n2, wrh, wrl, br, tril, s):
    t, d = x.shape
    tm = min(TM_MIX, s)
    per_b = s // tm
    const = lambda shape: pl.BlockSpec(shape, lambda i: (0,) * len(shape))
    tile = lambda n: pl.BlockSpec((tm, n), lambda i: (i, 0))
    return pl.pallas_call(
        _mix_body,
        grid=(t // tm,),
        in_specs=[tile(D_ATTN), tile(D_CONV), tile(d),
                  pl.BlockSpec((1, N_MOD, d), lambda i: (i // per_b, 0, 0)),
                  const((1, D_ATTN)), const((d, d)), const((1, d)), const((d, LANES)), const((d, LANES)),
                  const((1, LANES)), const((tm, tm))],
        out_specs=[tile(d), tile(d), tile(LANES), tile(LANES), const((1, LANES))],
        out_shape=[jax.ShapeDtypeStruct((t, d), f32), jax.ShapeDtypeStruct((t, d), f32),
                   jax.ShapeDtypeStruct((t, LANES), jnp.int32), jax.ShapeDtypeStruct((t, LANES), f32),
                   jax.ShapeDtypeStruct((1, LANES), f32)],
        scratch_shapes=[pltpu.VMEM((1, LANES), f32)],
        compiler_params=pltpu.CompilerParams(dimension_semantics=("arbitrary",), vmem_limit_bytes=VMEM_LIMIT),
        name="mix",
    )(attn, cv, x, mod, aw, wo, n2, wrh, wrl, br, tril)


def _row_copy(src, src_row, dst, dst_row, sem):
    return pltpu.make_async_copy(src.at[pl.ds(src_row, 1)], dst.at[pl.ds(dst_row, 1)], sem)


def _dispatch_body(dest_ref, h2_ref, xs_ref, sem):
    n = dest_ref.shape[0]
    base = pl.program_id(0) * (n // TOP_K)

    def issue(a, _):
        _row_copy(h2_ref, base + a // TOP_K, xs_ref, dest_ref[a], sem).start()
        return 0

    lax.fori_loop(0, n, issue, 0)

    def drain(a, _):
        _row_copy(h2_ref, 0, xs_ref, 0, sem).wait()
        return 0

    lax.fori_loop(0, n, drain, 0)


def _dispatch(dest, h2, n_rows, tm):
    t, d = h2.shape
    return pl.pallas_call(
        _dispatch_body,
        grid=(t // tm,),
        in_specs=[pl.BlockSpec((tm * TOP_K,), lambda i: (i,), memory_space=pltpu.SMEM),
                  pl.BlockSpec(memory_space=pl.ANY)],
        out_specs=pl.BlockSpec(memory_space=pl.ANY),
        out_shape=jax.ShapeDtypeStruct((n_rows, d), f32),
        scratch_shapes=[pltpu.SemaphoreType.DMA(())],
        compiler_params=pltpu.CompilerParams(dimension_semantics=("arbitrary",), has_side_effects=True),
        name="dispatch",
    )(dest, h2)


def _experts_body(be_ref, nused_ref, xs_ref, wgu_ref, bgu_ref, wd_ref, bd_ref, ys_ref, wgu_bf, wd_bf):
    i = pl.program_id(0)
    prev = be_ref[jnp.maximum(i - 1, 0)]

    @pl.when((i == 0) | (be_ref[i] != prev))
    def _():
        wgu_bf[...] = wgu_ref[0].astype(bf16)
        wd_bf[...] = wd_ref[0].astype(bf16)

    @pl.when(i < nused_ref[0])
    def _():
        gu = _dot(xs_ref[...].astype(bf16), wgu_bf[...]) + bgu_ref[0]
        gate = jnp.minimum(gu[:, :D_EXPERT], SWIGLU_LIMIT)
        up = jnp.clip(gu[:, D_EXPERT:], -SWIGLU_LIMIT, SWIGLU_LIMIT)
        act = gate * jax.nn.sigmoid(SWIGLU_ALPHA * gate) * (up + 1)
        ys_ref[...] = _dot(act.astype(bf16), wd_bf[...]) + bd_ref[0]


def _experts(block_e, n_used, xs, w_gate_up, b_gate_up, w_down, b_down, bm):
    n_rows, d = xs.shape
    nb = n_rows // bm
    de2 = w_gate_up.shape[2]
    last = lambda i, nu: jnp.minimum(i, nu[0] - 1)
    grid_spec = pltpu.PrefetchScalarGridSpec(
        num_scalar_prefetch=2,
        grid=(nb,),
        in_specs=[pl.BlockSpec((bm, d), lambda i, be, nu: (last(i, nu), 0)),
                  pl.BlockSpec((1, d, de2), lambda i, be, nu: (be[last(i, nu)], 0, 0)),
                  pl.BlockSpec((1, 1, de2), lambda i, be, nu: (be[last(i, nu)], 0, 0)),
                  pl.BlockSpec((1, de2 // 2, d), lambda i, be, nu: (be[last(i, nu)], 0, 0)),
                  pl.BlockSpec((1, 1, d), lambda i, be, nu: (be[last(i, nu)], 0, 0))],
        out_specs=pl.BlockSpec((bm, d), lambda i, be, nu: (last(i, nu), 0)),
        scratch_shapes=[pltpu.VMEM((d, de2), bf16), pltpu.VMEM((de2 // 2, d), bf16)],
    )
    return pl.pallas_call(
        _experts_body,
        grid_spec=grid_spec,
        out_shape=jax.ShapeDtypeStruct((n_rows, d), f32),
        compiler_params=pltpu.CompilerParams(dimension_semantics=("arbitrary",), vmem_limit_bytes=VMEM_LIMIT),
        name="experts",
    )(block_e, n_used, xs, w_gate_up, b_gate_up, w_down, b_down)


def _combine_body(dest_ref, rw_ref, x1_ref, mod_ref, ys_ref, o_ref, buf, sem):
    tm = x1_ref.shape[0]
    n = dest_ref.shape[0]

    def issue(a, _):
        _row_copy(ys_ref, dest_ref[a], buf.at[a % TOP_K], a // TOP_K, sem).start()
        return 0

    lax.fori_loop(0, n, issue, 0)

    def drain(a, _):
        _row_copy(ys_ref, 0, buf.at[0], 0, sem).wait()
        return 0

    lax.fori_loop(0, n, drain, 0)

    g2 = mod_ref[0, 5:6, :]
    rw = rw_ref[...]
    acc = jnp.zeros((tm, x1_ref.shape[1]), f32)
    for kk in range(TOP_K):
        acc = acc + rw[:, kk:kk + 1] * buf[kk]
    o_ref[...] = x1_ref[...] + g2 * acc


def _combine(dest, rw, x1, mod, ys, s, tm):
    t, d = x1.shape
    per_b = s // tm
    return pl.pallas_call(
        _combine_body,
        grid=(t // tm,),
        in_specs=[pl.BlockSpec((tm * TOP_K,), lambda i: (i,), memory_space=pltpu.SMEM),
                  pl.BlockSpec((tm, LANES), lambda i: (i, 0)),
                  pl.BlockSpec((tm, d), lambda i: (i, 0)),
                  pl.BlockSpec((1, N_MOD, d), lambda i: (i // per_b, 0, 0)),
                  pl.BlockSpec(memory_space=pl.ANY)],
        out_specs=pl.BlockSpec((tm, d), lambda i: (i, 0)),
        out_shape=jax.ShapeDtypeStruct((t, d), f32),
        scratch_shapes=[pltpu.VMEM((TOP_K, tm, d), f32), pltpu.SemaphoreType.DMA(())],
        compiler_params=pltpu.CompilerParams(dimension_semantics=("arbitrary",), vmem_limit_bytes=VMEM_LIMIT),
        name="combine",
    )(dest, rw, x1, mod, ys)


def _layer(x, c_pad, w_ada, b_ada, norm1_w, w_in, b_forget, q_norm_w, k_norm_w, conv_w, attn_out_norm_w,
           conv_out_norm_w, w_out, norm2_w, w_router, b_router, w_gate_up, b_gate_up, w_down, b_down):
    b, s, d = x.shape
    t = b * s
    mod = _ada(c_pad, w_ada, b_ada[None, :])[:b].reshape(b, N_MOD, d)

    nqkv = 3 * D_ATTN
    w_main = jnp.concatenate([w_in[:, :nqkv], w_in[:, nqkv + ATTN_HEADS:]], axis=1).astype(bf16)
    w_f = jnp.tile(w_in[:, nqkv:nqkv + ATTN_HEADS], (1, N_BIAS_TERMS))
    w_f = jnp.pad(w_f, ((0, 0), (0, LANES - N_BIAS_TERMS * ATTN_HEADS))).astype(bf16)
    b_f = jnp.pad(jnp.tile(b_forget, N_BIAS_TERMS), (0, LANES - N_BIAS_TERMS * ATTN_HEADS))[None, :]
    hid = jnp.arange(D_ATTN) // HEAD_DIM
    g_mat = jnp.where(hid[:, None] == hid[None, :], 1.0 / HEAD_DIM, 0.0).astype(bf16)
    qw = jnp.tile(q_norm_w, ATTN_HEADS)[None, :]
    kw = jnp.tile(k_norm_w, ATTN_HEADS)[None, :]
    tm1 = min(TM_INPROJ, s)
    r1 = jnp.arange(tm1)
    tri = (r1[None, :] <= r1[:, None]).astype(bf16)

    q, k, v, eq, ek, cv = _inproj(x, mod, norm1_w[None, :], w_main, w_f, b_f, g_mat, qw, kw, conv_w,
                                  conv_out_norm_w[None, :], tri)
    attn = _attention(q, k, v, eq, ek)

    tm2 = min(TM_MIX, s)
    r2 = jnp.arange(tm2)
    tril = (r2[None, :] < r2[:, None]).astype(bf16)
    wr = jnp.pad(w_router, ((0, 0), (0, LANES - N_EXPERTS)))
    wrh = wr.astype(bf16)
    wrl = (wr - wrh.astype(f32)).astype(bf16)
    br = jnp.pad(b_router, (0, LANES - N_EXPERTS), constant_values=NEG)[None, :]
    x1, h2, ri, rw, cnt = _mix(attn.reshape(t, D_ATTN), cv.reshape(t, D_CONV), x.reshape(t, d), mod,
                               attn_out_norm_w[None, :], w_out.astype(bf16), norm2_w[None, :], wrh, wrl, br, tril, s)

    bm = min(BM, t)
    counts = cnt[0, :N_EXPERTS].astype(jnp.int32)
    padded = (counts + bm - 1) // bm * bm
    pad_end = jnp.cumsum(padded)
    pad_start = pad_end - padded
    dest = (pad_start[ri[:, :TOP_K]] + ri[:, TOP_K:2 * TOP_K]).reshape(t * TOP_K)
    nb = t * TOP_K // bm + N_EXPERTS
    n_used = (pad_end[-1] // bm).astype(jnp.int32)
    block_e = jnp.searchsorted(pad_end, jnp.arange(nb, dtype=jnp.int32) * bm, side='right').astype(jnp.int32)
    block_e = jnp.minimum(block_e, N_EXPERTS - 1)

    tm3 = min(TM_ROUTE, s)
    xs = _dispatch(dest, h2, nb * bm, tm3)
    ys = _experts(block_e, n_used[None], xs, w_gate_up, b_gate_up[:, None, :], w_down, b_down[:, None, :], bm)
    out = _combine(dest, rw, x1, mod, ys, s, tm3)
    return out.reshape(b, s, d)


def kernel(x, c, w_ada, b_ada, norm1_w, w_in, b_forget, q_norm_w, k_norm_w, conv_w, attn_out_norm_w,
           conv_out_norm_w, w_out, norm2_w, w_router, b_router, w_gate_up, b_gate_up, w_down, b_down):
    depth = w_ada.shape[0]
    rows = max(16, c.shape[0])
    c_pad = jnp.pad(c, ((0, rows - c.shape[0]), (0, 0)))
    for l in range(depth):
        x = _layer(x, c_pad, w_ada[l], b_ada[l], norm1_w[l], w_in[l], b_forget[l], q_norm_w[l], k_norm_w[l],
                   conv_w[l], attn_out_norm_w[l], conv_out_norm_w[l], w_out[l], norm2_w[l], w_router[l],
                   b_router[l], w_gate_up[l], b_gate_up[l], w_down[l], b_down[l])
    return x
```
